```python
import math
import numpy as np
import jax
import jax.numpy as jnp
from jax import lax

D_MODEL = 1024
BATCH = 16
SEQ = 4096
DEPTH = 4

GRID_W = 64
CTX_LEN = 256
EPS = 1e-6
CONV_W = 3

SSD_HEADS = 8
SSD_HEAD_DIM = 64
SSD_GROUPS = 2
SSD_STATE = 64
SSD_CHUNK = 128
SSD_HPG = SSD_HEADS // SSD_GROUPS
SSD_INNER = SSD_HEADS * SSD_HEAD_DIM
SSD_CONV_DIM = SSD_INNER + 2 * SSD_GROUPS * SSD_STATE

NA_HEADS = 4
NA_HEAD_DIM = 64
NA_INNER = NA_HEADS * NA_HEAD_DIM
NA_ROWS = 8
NA_COLS = 16
NA_QBLOCK = 16
NA_KBLOCK = NA_QBLOCK + NA_COLS

HG_HEADS = 4
HG_KDIM = 64
HG_VDIM = 64
HG_K = HG_HEADS * HG_KDIM
HG_V = HG_HEADS * HG_VDIM
HG_CHUNK = 64

D_MIX = SSD_INNER + NA_INNER + HG_V
D_FF = 2816
IN_WIDTHS = (SSD_INNER, SSD_CONV_DIM, 2 * SSD_HEADS, 3 * NA_INNER, HG_K, 2 * HG_K, HG_V, HG_V)
D_IN = sum(IN_WIDTHS)

kernel_name = 'hybrid_ssd_natten_hgrn2_dit'


def rmsnorm(x, w):
    xf = x.astype(jnp.float32)
    y = xf * lax.rsqrt(jnp.mean(xf * xf, axis=-1, keepdims=True) + EPS)
    return (y * w.astype(jnp.float32)).astype(x.dtype)


def modulate(h, shift, scale):
    return h * (1 + scale) + shift


def dwconv_centred(x, w, b):
    k_w = w.shape[0]
    pad = k_w // 2
    length = x.shape[1]
    xp = jnp.pad(x, ((0, 0), (pad, pad), (0, 0)))
    y = b + xp[:, 0:length] * w[0]
    for k in range(1, k_w):
        y = y + xp[:, k:k + length] * w[k]
    return y


def _flip(t):
    return jnp.flip(t, axis=1)


def _to_chunks(t, chunk):
    bsz, length = t.shape[:2]
    return jnp.moveaxis(t.reshape((bsz, length // chunk, chunk) + t.shape[2:]), 1, 0)


def _from_chunks(t):
    nc, bsz, chunk = t.shape[:3]
    return jnp.moveaxis(t, 0, 1).reshape((bsz, nc * chunk) + t.shape[3:])


def ssd_scan(xdt, bm, cm, loga, h0):
    causal = np.tril(np.ones((SSD_CHUNK, SSD_CHUNK), bool))[None, :, :, None, None]

    def step(h, inp):
        x_c, b_c, c_c, a_c = inp
        cum = jnp.cumsum(a_c, axis=1)
        seg = jnp.where(causal, cum[:, :, None] - cum[:, None], -jnp.inf)
        decay = jnp.exp(seg)
        scores = jnp.einsum('btgn,bsgn->btsg', c_c, b_c)
        y = jnp.einsum('btsg,btsgh,bsghp->btghp', scores, decay, x_c)
        y = y + jnp.einsum('btgn,bghnp->btghp', c_c, h) * jnp.exp(cum)[..., None]
        to_end = jnp.exp(cum[:, -1:] - cum)
        h = jnp.exp(cum[:, -1])[..., None, None] * h + jnp.einsum('bsgn,bsgh,bsghp->bghnp', b_c, to_end, x_c)
        return h, y

    h, ys = lax.scan(step, h0, (_to_chunks(xdt, SSD_CHUNK), _to_chunks(bm, SSD_CHUNK),
                                _to_chunks(cm, SSD_CHUNK), _to_chunks(loga, SSD_CHUNK)))
    return _from_chunks(ys), h


def gla_scan(q, k, v, log_f, s0):
    causal = np.tril(np.ones((HG_CHUNK, HG_CHUNK), bool))[None, :, :, None, None]

    def step(s, inp):
        q_c, k_c, v_c, g_c = inp
        cum = jnp.cumsum(g_c, axis=1)
        seg = jnp.where(causal, cum[:, :, None] - cum[:, None], -jnp.inf)
        scores = jnp.einsum('bthk,bshk,btshk->bhts', q_c, k_c, jnp.exp(seg))
        o = jnp.einsum('bhts,bshv->bthv', scores, v_c)
        o = o + jnp.einsum('bthk,bhkv->bthv', q_c * jnp.exp(cum), s)
        s = jnp.exp(cum[:, -1])[..., None] * s + jnp.einsum('bshk,bshv->bhkv', k_c * jnp.exp(cum[:, -1:] - cum), v_c)
        return s, o

    s, os_ = lax.scan(step, s0, (_to_chunks(q, HG_CHUNK), _to_chunks(k, HG_CHUNK),
                                 _to_chunks(v, HG_CHUNK), _to_chunks(log_f, HG_CHUNK)))
    return _from_chunks(os_), s


def ssd_mixer(parts_ctx, parts_lat, conv_w, conv_b, dt_bias, a_log, d_skip, norm_w, need_ctx):
    a_neg = -jnp.exp(a_log.astype(jnp.float32))

    def prep(parts):
        z, xbc, dt_raw = parts
        bsz, length, _ = z.shape
        xbc = jax.nn.silu(dwconv_centred(xbc, conv_w, conv_b)).astype(jnp.float32)
        xs, bm, cm = jnp.split(xbc, [SSD_INNER, SSD_INNER + SSD_GROUPS * SSD_STATE], axis=-1)
        xs = xs.reshape(bsz, length, SSD_GROUPS, SSD_HPG, SSD_HEAD_DIM)
        bm = bm.reshape(bsz, length, SSD_GROUPS, SSD_STATE)
        cm = cm.reshape(bsz, length, SSD_GROUPS, SSD_STATE)
        dt = jax.nn.softplus(dt_raw.astype(jnp.float32).reshape(bsz, length, 2, SSD_HEADS)
                             + dt_bias.astype(jnp.float32))
        loga = (dt * a_neg).reshape(bsz, length, 2, SSD_GROUPS, SSD_HPG)
        dt = dt.reshape(bsz, length, 2, SSD_GROUPS, SSD_HPG)
        fwd = (xs * dt[:, :, 0, :, :, None], bm, cm, loga[:, :, 0])
        bwd = tuple(_flip(t) for t in (xs * dt[:, :, 1, :, :, None], bm, cm, loga[:, :, 1]))
        return z, xs, fwd, bwd

    z_c, xs_c, fwd_c, bwd_c = prep(parts_ctx)
    z_l, xs_l, fwd_l, bwd_l = prep(parts_lat)
    h0 = jnp.zeros((z_l.shape[0], SSD_GROUPS, SSD_HPG, SSD_STATE, SSD_HEAD_DIM), jnp.float32)
    yf_c, hf = ssd_scan(*fwd_c, h0)
    yb_c, hb = ssd_scan(*bwd_c, h0)
    yf_l, _ = ssd_scan(*fwd_l, hf)
    yb_l, _ = ssd_scan(*bwd_l, hb)
    d_h = d_skip.astype(jnp.float32).reshape(SSD_GROUPS, SSD_HPG, 1)

    def finish(z, xs, yf, yb):
        y = (yf + _flip(yb) + xs * d_h).reshape(z.shape).astype(z.dtype)
        return rmsnorm(y * jax.nn.silu(z), norm_w)

    y_ctx = finish(z_c, xs_c, yf_c, yb_c) if need_ctx else None
    return y_ctx, finish(z_l, xs_l, yf_l, yb_l)


def na_mixer(qkv_ctx, qkv_lat, rpb, need_ctx):
    scale = NA_HEAD_DIM ** -0.5

    def heads(t):
        bsz, length, _ = t.shape
        return t.reshape(bsz, length, NA_HEADS, NA_HEAD_DIM).transpose(0, 2, 1, 3)

    q_c, k_c, v_c = (heads(t) for t in jnp.split(qkv_ctx, 3, axis=-1))
    q_l, k_l, v_l = (heads(t) for t in jnp.split(qkv_lat, 3, axis=-1))
    o_ctx = None
    if need_ctx:
        s = jnp.einsum('bhqd,bhkd->bhqk', q_c, k_c).astype(jnp.float32) * scale
        o = jnp.einsum('bhqk,bhkd->bhqd', jax.nn.softmax(s, axis=-1).astype(v_c.dtype), v_c)
        o_ctx = o.transpose(0, 2, 1, 3).reshape(o.shape[0], o.shape[2], NA_INNER)

    bsz, _, length, _ = q_l.shape
    rows = length // GRID_W
    kr = min(NA_ROWS, rows)
    nb = GRID_W // NA_QBLOCK
    qcol = np.arange(GRID_W).reshape(nb, NA_QBLOCK)
    qcol_start = np.clip(qcol - NA_COLS // 2, 0, GRID_W - NA_COLS)
    kcol = (np.clip(np.arange(nb) * NA_QBLOCK - NA_COLS // 2, 0, GRID_W - NA_KBLOCK)[:, None]
            + np.arange(NA_KBLOCK))
    in_win = ((kcol[:, None, :] >= qcol_start[:, :, None])
              & (kcol[:, None, :] < qcol_start[:, :, None] + NA_COLS))
    col_idx = np.clip(kcol[:, None, :] - qcol[:, :, None] + NA_COLS - 1, 0, 2 * NA_COLS - 2)
    mask = jnp.asarray(in_win[:, :, None, :])
    rpb32 = rpb.astype(jnp.float32)
    k_grid = k_l.reshape(bsz, NA_HEADS, rows, GRID_W, NA_HEAD_DIM)
    v_grid = v_l.reshape(bsz, NA_HEADS, rows, GRID_W, NA_HEAD_DIM)
    q_rows = jnp.moveaxis(q_l.reshape(bsz, NA_HEADS, rows, nb, NA_QBLOCK, NA_HEAD_DIM), 2, 0)
    n_loc = kr * NA_KBLOCK

    def one_row(args):
        i, q = args
        r0 = jnp.clip(i - kr // 2, 0, rows - kr)
        k_blk = lax.dynamic_slice_in_dim(k_grid, r0, kr, axis=2)[:, :, :, kcol]
        v_blk = lax.dynamic_slice_in_dim(v_grid, r0, kr, axis=2)[:, :, :, kcol]
        bias = rpb32[:, r0 + jnp.arange(kr) - i + NA_ROWS - 1][:, :, col_idx]
        bias = bias.transpose(0, 2, 3, 1, 4)
        s_loc = jnp.einsum('bhnqd,bhrnkd->bhnqrk', q, k_blk).astype(jnp.float32) * scale
        s_loc = jnp.where(mask, s_loc + bias, -jnp.inf)
        s_ctx = jnp.einsum('bhnqd,bhcd->bhnqc', q, k_c).astype(jnp.float32) * scale
        s_all = jnp.concatenate([s_loc.reshape(s_loc.shape[:4] + (n_loc,)), s_ctx], axis=-1)
        p = jax.nn.softmax(s_all, axis=-1).astype(q.dtype)
        p_loc = p[..., :n_loc].reshape(s_loc.shape)
        return (jnp.einsum('bhnqrk,bhrnkd->bhnqd', p_loc, v_blk)
                + jnp.einsum('bhnqc,bhcd->bhnqd', p[..., n_loc:], v_c))

    o_rows = lax.map(one_row, (jnp.arange(rows), q_rows))
    o_lat = jnp.moveaxis(o_rows, 0, 2).reshape(bsz, NA_HEADS, length, NA_HEAD_DIM)
    o_lat = o_lat.transpose(0, 2, 1, 3).reshape(bsz, length, NA_INNER)
    return o_ctx, o_lat


def hgrn_mixer(parts_ctx, parts_lat, lb, norm_w, need_ctx):
    lb_h = lb.reshape(HG_HEADS, HG_KDIM)
    log_lb = jnp.log(lb_h)
    log_1mlb = jnp.log1p(-lb_h)

    def prep(parts):
        q, f, i, g = parts
        bsz, length, _ = q.shape
        q = jax.nn.silu(q.astype(jnp.float32)).reshape(bsz, length, HG_HEADS, HG_KDIM)
        f = f.astype(jnp.float32).reshape(bsz, length, 2, HG_HEADS, HG_KDIM)
        log_f = jnp.logaddexp(log_lb, log_1mlb + jax.nn.log_sigmoid(f))
        k = (1.0 - lb_h) * jax.nn.sigmoid(-f)
        v = i.astype(jnp.float32).reshape(bsz, length, HG_HEADS, HG_VDIM)
        fwd = (q, k[:, :, 0], v, log_f[:, :, 0])
        bwd = tuple(_flip(t) for t in (q, k[:, :, 1], v, log_f[:, :, 1]))
        return g, fwd, bwd

    g_c, fwd_c, bwd_c = prep(parts_ctx)
    g_l, fwd_l, bwd_l = prep(parts_lat)
    s0 = jnp.zeros((g_l.shape[0], HG_HEADS, HG_KDIM, HG_VDIM), jnp.float32)
    of_c, sf = gla_scan(*fwd_c, s0)
    ob_c, sb = gla_scan(*bwd_c, s0)
    of_l, _ = gla_scan(*fwd_l, sf)
    ob_l, _ = gla_scan(*bwd_l, sb)

    def finish(g, of, ob):
        o = (of + _flip(ob)).astype(g.dtype)
        o = rmsnorm(o, norm_w).reshape(g.shape)
        return o * jax.nn.silu(g)

    o_ctx = finish(g_c, of_c, ob_c) if need_ctx else None
    return o_ctx, finish(g_l, of_l, ob_l)


def mixer_block(u_ctx, u_lat, w_in, w_out, ssd_conv_w, ssd_conv_b, ssd_dt_bias, ssd_a_log, ssd_d,
                ssd_norm, na_rpb, hg_lb, hg_norm, need_ctx):
    cuts = [int(v) for v in np.cumsum(IN_WIDTHS)[:-1]]
    pc = jnp.split(u_ctx @ w_in, cuts, axis=-1)
    pl = jnp.split(u_lat @ w_in, cuts, axis=-1)
    ssd_c, ssd_l = ssd_mixer(pc[0:3], pl[0:3], ssd_conv_w, ssd_conv_b, ssd_dt_bias, ssd_a_log, ssd_d,
                             ssd_norm, need_ctx)
    na_c, na_l = na_mixer(pc[3], pl[3], na_rpb, need_ctx)
    hg_c, hg_l = hgrn_mixer(pc[4:8], pl[4:8], hg_lb, hg_norm, need_ctx)
    out_lat = jnp.concatenate([ssd_l, na_l, hg_l], axis=-1) @ w_out
    out_ctx = jnp.concatenate([ssd_c, na_c, hg_c], axis=-1) @ w_out if need_ctx else None
    return out_ctx, out_lat


def conv_ffn(h, w_up, conv_w, conv_b, w_down):
    gate, up = jnp.split(h @ w_up, 2, axis=-1)
    gate = dwconv_centred(gate, conv_w, conv_b)
    return (jax.nn.silu(gate) * up) @ w_down


def setup_inputs(seed: int = 0) -> dict:
    key = jax.random.key(seed)
    ks = jax.random.split(key, 26)
    f32 = jnp.float32

    def nrm(k, shape, scale):
        return jax.random.normal(k, shape, f32) * scale

    def gain(k, shape):
        return 1.0 + 0.05 * jax.random.normal(k, shape, f32)

    dt0 = jnp.exp(jax.random.uniform(ks[11], (DEPTH, 2, SSD_HEADS), f32, math.log(1e-3), math.log(1e-1)))
    return {
        'x': nrm(ks[0], (BATCH, SEQ, D_MODEL), 1.0),
        'c': nrm(ks[1], (BATCH, D_MODEL), 1.0),
        'ctx': nrm(ks[2], (BATCH, CTX_LEN, D_MODEL), 1.0),
        'c_ctx': nrm(ks[3], (D_MODEL,), 1.0),
        'ada_w': nrm(ks[4], (DEPTH, D_MODEL, 6 * D_MODEL), 0.5 * D_MODEL ** -0.5),
        'ada_b': nrm(ks[5], (DEPTH, 6 * D_MODEL), 0.02),
        'norm_mix_pre': gain(ks[6], (DEPTH, D_MODEL)),
        'norm_mix_post': gain(ks[7], (DEPTH, D_MODEL)),
        'norm_ffn_pre': gain(ks[8], (DEPTH, D_MODEL)),
        'norm_ffn_post': gain(ks[9], (DEPTH, D_MODEL)),
        'w_in': nrm(ks[10], (DEPTH, D_MODEL, D_IN), D_MODEL ** -0.5),
        'ssd_conv_w': nrm(ks[12], (DEPTH, CONV_W, SSD_CONV_DIM), CONV_W ** -0.5),
        'ssd_conv_b': nrm(ks[13], (DEPTH, SSD_CONV_DIM), 0.02),
        'ssd_dt_bias': dt0 + jnp.log(-jnp.expm1(-dt0)),
        'ssd_a_log': jnp.log(jax.random.uniform(ks[14], (DEPTH, 2, SSD_HEADS), f32, 1.0, 16.0)),
        'ssd_d': 1.0 + 0.1 * jax.random.normal(ks[15], (DEPTH, SSD_HEADS), f32),
        'ssd_norm': gain(ks[16], (DEPTH, SSD_INNER)),
        'na_rpb': nrm(ks[17], (DEPTH, NA_HEADS, 2 * NA_ROWS - 1, 2 * NA_COLS - 1), 0.1),
        'hg_lb_logits': nrm(ks[18], (DEPTH, HG_K), 0.1),
        'hg_norm': gain(ks[19], (DEPTH, HG_VDIM)),
        'w_out': nrm(ks[20], (DEPTH, D_MIX, D_MODEL), D_MIX ** -0.5),
        'ffn_w_up': nrm(ks[21], (DEPTH, D_MODEL, 2 * D_FF), D_MODEL ** -0.5),
        'ffn_conv_w': nrm(ks[22], (DEPTH, CONV_W, D_FF), CONV_W ** -0.5),
        'ffn_conv_b': nrm(ks[23], (DEPTH, D_FF), 0.02),
        'ffn_w_down': nrm(ks[24], (DEPTH, D_FF, D_MODEL), D_FF ** -0.5),
    }


def reference(x, c, ctx, c_ctx, ada_w, ada_b, norm_mix_pre, norm_mix_post, norm_ffn_pre, norm_ffn_post,
              w_in, ssd_conv_w, ssd_conv_b, ssd_dt_bias, ssd_a_log, ssd_d, ssd_norm, na_rpb,
              hg_lb_logits, hg_norm, w_out, ffn_w_up, ffn_conv_w, ffn_conv_b, ffn_w_down):
    lb_all = jnp.cumsum(jax.nn.softmax(hg_lb_logits.astype(jnp.float32), axis=0), axis=0)
    lb_all = lb_all - lb_all[0]
    s_lat = jax.nn.silu(c)[:, None, :]
    s_ctx = jax.nn.silu(c_ctx)[None, None, :]
    h_lat, h_ctx = x, ctx
    for l in range(DEPTH):
        need_ctx = l < DEPTH - 1
        m_lat = jnp.split(s_lat @ ada_w[l] + ada_b[l], 6, axis=-1)
        m_ctx = jnp.split(s_ctx @ ada_w[l] + ada_b[l], 6, axis=-1)
        u_ctx = modulate(rmsnorm(h_ctx, norm_mix_pre[l]), m_ctx[0], m_ctx[1])
        u_lat = modulate(rmsnorm(h_lat, norm_mix_pre[l]), m_lat[0], m_lat[1])
        a_ctx, a_lat = mixer_block(u_ctx, u_lat, w_in[l], w_out[l], ssd_conv_w[l], ssd_conv_b[l],
                                   ssd_dt_bias[l], ssd_a_log[l], ssd_d[l], ssd_norm[l], na_rpb[l],
                                   lb_all[l], hg_norm[l], need_ctx)
        h_lat = h_lat + m_lat[2] * rmsnorm(a_lat, norm_mix_post[l])
        v_lat = modulate(rmsnorm(h_lat, norm_ffn_pre[l]), m_lat[3], m_lat[4])
        f_lat = conv_ffn(v_lat, ffn_w_up[l], ffn_conv_w[l], ffn_conv_b[l], ffn_w_down[l])
        h_lat = h_lat + m_lat[5] * rmsnorm(f_lat, norm_ffn_post[l])
        if need_ctx:
            h_ctx = h_ctx + m_ctx[2] * rmsnorm(a_ctx, norm_mix_post[l])
            v_ctx = modulate(rmsnorm(h_ctx, norm_ffn_pre[l]), m_ctx[3], m_ctx[4])
            f_ctx = conv_ffn(v_ctx, ffn_w_up[l], ffn_conv_w[l], ffn_conv_b[l], ffn_w_down[l])
            h_ctx = h_ctx + m_ctx[5] * rmsnorm(f_ctx, norm_ffn_post[l])
    return h_lat
```

```python
import functools

import numpy as np
import jax
import jax.numpy as jnp
from jax import lax
from jax.experimental import pallas as pl
from jax.experimental.pallas import tpu as pltpu

F32 = jnp.float32
BF16 = jnp.bfloat16

EPS = 1e-6
TILE = 256
GRID_W = 64
LANES = 128
SUBLANES = 8
NEG = -1e30

SSD_HEADS = 8
SSD_HEAD_DIM = 64
SSD_STATE = 64
SSD_INNER = SSD_HEADS * SSD_HEAD_DIM
SSD_BC = 2 * SSD_STATE
SSD_CONV_DIM = SSD_INNER + 2 * SSD_BC
SSD_CHUNK = 128
NA_HEADS = 4
NA_HEAD_DIM = 64
NA_INNER = NA_HEADS * NA_HEAD_DIM
NA_ROWS = 8
NA_COLS = 16
HG_INNER = 256
HG_HEAD = 64
HG_BLOCK = 16
D_FF = 2816
IN_WIDTHS = (512, 768, 16, 768, 256, 512, 256, 256)
VMEM_LIMIT = 56 * 1024 * 1024


def _dot(a, b):
    return jnp.dot(a, b, preferred_element_type=F32)


def _dot_nt(a, b):
    return lax.dot_general(a, b, (((1,), (1,)), ((), ())), preferred_element_type=F32)


def _dot_tn(a, b):
    return lax.dot_general(a, b, (((0,), (0,)), ((), ())), preferred_element_type=F32)


def _split3(x):
    hi = x.astype(BF16)
    r = x - hi.astype(F32)
    mid = r.astype(BF16)
    lo = (r - mid.astype(F32)).astype(BF16)
    return hi, mid, lo


def _dot01(sel, x):
    hi, mid, lo = _split3(x)
    return _dot(sel, hi) + _dot(sel, mid) + _dot(sel, lo)


def _dot01_r(x, sel):
    hi, mid, lo = _split3(x)
    return _dot(hi, sel) + _dot(mid, sel) + _dot(lo, sel)


def _silu(x):
    return x / (1.0 + jnp.exp(-x))


def _softplus(x):
    return jnp.maximum(x, 0.0) + jnp.log(1.0 + jnp.exp(-jnp.abs(x)))


def _log_sigmoid(x):
    return jnp.minimum(x, 0.0) - jnp.log(1.0 + jnp.exp(-jnp.abs(x)))


def _rms(x, w):
    ms = jnp.mean(x * x, axis=-1, keepdims=True)
    return x * lax.rsqrt(ms + EPS) * w


def _iota(shape, dim):
    return lax.broadcasted_iota(jnp.int32, shape, dim)


def _params(n_grid=2, carried=False):
    sem = ("parallel",) * (n_grid - 1) + (("arbitrary",) if carried else ("parallel",))
    return pltpu.CompilerParams(dimension_semantics=sem, vmem_limit_bytes=VMEM_LIMIT)


def _const_spec(shape):
    nd = len(shape)
    return pl.BlockSpec(shape, lambda *_: (0,) * nd)


def _mod_spec(first_tile):
    def idx(b, t, n_b):
        return (jnp.where(t + first_tile == 0, n_b, b), 0, 0)
    return idx


def _ada_kernel(c_ref, w_ref, b_ref, o_ref):
    s = _silu(c_ref[...]).astype(BF16)
    o_ref[...] = _dot(s, w_ref[...].astype(BF16)) + b_ref[...]


def _ada_call(cvec, ada_w, ada_b):
    depth, d, n6 = ada_w.shape
    rows = cvec.shape[0]
    nblk = n6 // d
    return pl.pallas_call(
        _ada_kernel,
        grid=(depth, nblk),
        in_specs=[
            pl.BlockSpec((rows, d), lambda l, j: (0, 0)),
            pl.BlockSpec((None, d, d), lambda l, j: (l, 0, j)),
            pl.BlockSpec((None, 1, d), lambda l, j: (l, 0, j)),
        ],
        out_specs=pl.BlockSpec((None, rows, d), lambda l, j: (l, 0, j)),
        out_shape=jax.ShapeDtypeStruct((depth, rows, n6), F32),
        compiler_params=_params(2),
        name="adaln",
    )(cvec, ada_w, ada_b.reshape(depth, 1, n6))


_COLS = dict(z=(0, 512), xbc=(512, 1280), dt=(1280, 1408), q=(1408, 1664), k=(1664, 1920),
             v=(1920, 2176), hq=(2176, 2432), hf=(2432, 2944), hi=(2944, 3200), hg=(3200, 3456))
_N_IN = 3456
_IN_OUTS = (("z", BF16), ("xbc", BF16), ("dt", F32), ("q", BF16), ("k", BF16), ("v", BF16),
            ("hq", BF16), ("hf", F32), ("hi", BF16), ("hg", BF16))


def _inproj_kernel(h_ref, sh_ref, sc_ref, nw_ref, w_ref, *outs):
    u = _rms(h_ref[...], nw_ref[...]) * (1.0 + sc_ref[...]) + sh_ref[...]
    u = u.astype(BF16)
    for (name, _), o_ref in zip(_IN_OUTS, outs):
        a, b = _COLS[name]
        o_ref[...] = _dot(u, w_ref[:, a:b]).astype(o_ref.dtype)


def _inproj_call(h, shift, scale, nw, w_all, n_b):
    bsz, s_tot, d = h.shape
    nt = s_tot // TILE
    mod = _mod_spec(0)
    tok = lambda width: pl.BlockSpec((None, TILE, width), lambda b, t: (b, t, 0))
    return pl.pallas_call(
        _inproj_kernel,
        grid=(bsz, nt),
        in_specs=[
            tok(d),
            pl.BlockSpec((None, 1, d), lambda b, t: mod(b, t, n_b)),
            pl.BlockSpec((None, 1, d), lambda b, t: mod(b, t, n_b)),
            _const_spec((1, d)),
            _const_spec((d, _N_IN)),
        ],
        out_specs=[tok(_COLS[n][1] - _COLS[n][0]) for n, _ in _IN_OUTS],
        out_shape=[jax.ShapeDtypeStruct((bsz, s_tot, _COLS[n][1] - _COLS[n][0]), dt)
                   for n, dt in _IN_OUTS],
        compiler_params=_params(2),
        name="inproj",
    )(h, shift, scale, nw, w_all)


def _scan_tile(direction, c, nt):
    if direction == 0:
        return c
    return jnp.where(c == 0, 0, nt - c)


def _halo_maps(direction, nt):
    rpt = TILE // SUBLANES

    def prev_idx(b, c):
        t = _scan_tile(direction, c, nt)
        has = jnp.logical_and(t != 0, t != 1)
        return (b, jnp.where(has, t * rpt - 1, t * rpt), 0)

    def next_idx(b, c):
        t = _scan_tile(direction, c, nt)
        has = jnp.logical_and(t != 0, t != nt - 1)
        return (b, jnp.where(has, (t + 1) * rpt, t * rpt + rpt - 1), 0)

    return prev_idx, next_idx


def _seq_edges(t, nt):
    has_prev = jnp.logical_and(t != 0, t != 1)
    has_next = jnp.logical_and(t != 0, t != nt - 1)
    return has_prev, has_next


def _ssd_kernel(direction, *refs):
    if direction == 0:
        (x_ref, xp_ref, xn_ref, dt_ref, cw_ref, cb_ref, dtb_ref, an_ref, y_ref, h_scr) = refs
    else:
        (x_ref, xp_ref, xn_ref, dt_ref, cw_ref, cb_ref, dtb_ref, an_ref,
         yf_ref, z_ref, dsk_ref, nw_ref, y_ref, h_scr, yb_scr) = refs
    c = pl.program_id(1)
    nt = pl.num_programs(1)
    t = _scan_tile(direction, c, nt)
    has_prev, has_next = _seq_edges(t, nt)

    @pl.when(c == 0)
    def _():
        h_scr[...] = jnp.zeros_like(h_scr)

    x = x_ref[...].astype(F32)
    prow = jnp.where(has_prev, xp_ref[SUBLANES - 1:SUBLANES, :].astype(F32), 0.0)
    nrow = jnp.where(has_next, xn_ref[0:1, :].astype(F32), 0.0)
    ridx = _iota((TILE, 1), 0)
    x_prev = jnp.where(ridx == 0, prow, pltpu.roll(x, 1, 0))
    x_next = jnp.where(ridx == TILE - 1, nrow, pltpu.roll(x, TILE - 1, 0))
    xc = _silu(cb_ref[...] + cw_ref[0:1, :] * x_prev + cw_ref[1:2, :] * x + cw_ref[2:3, :] * x_next)

    dt = _softplus(dt_ref[...] + dtb_ref[...])
    loga = dt * an_ref[...]

    ln = SSD_CHUNK
    r_i = _iota((ln, ln), 0)
    c_i = _iota((ln, ln), 1)
    if direction == 0:
        tri = (c_i <= r_i)
    else:
        tri = (c_i >= r_i)
    tri_b = tri.astype(BF16)
    lane = _iota((1, LANES), 1)
    lo = lane < SSD_HEAD_DIM
    blk = (_iota((LANES, LANES), 0) // SSD_HEAD_DIM) == (_iota((LANES, LANES), 1) // SSD_HEAD_DIM)
    rlo = _iota((LANES, 1), 0) < SSD_HEAD_DIM

    subs = (0, 1) if direction == 0 else (1, 0)
    for j in subs:
        sl = slice(j * ln, (j + 1) * ln)
        cum = _dot01(tri_b, loga[sl])
        cum_t = cum.T
        dt_t = dt[sl].T
        tot = cum[ln - 1:ln, :] if direction == 0 else cum[0:1, :]
        dt_s = dt[sl]
        for g in range(2):
            bm = xc[sl, SSD_INNER:SSD_INNER + SSD_BC]
            cm = xc[sl, SSD_INNER + SSD_BC:SSD_CONV_DIM]
            keep = lo if g == 0 else jnp.logical_not(lo)
            bg = jnp.where(keep, bm, pltpu.roll(bm, SSD_STATE, 1))
            cg = jnp.where(keep, cm, pltpu.roll(cm, SSD_STATE, 1))
            sg = _dot_nt(jnp.where(lo, cg, 0.0).astype(BF16), bg.astype(BF16))
            for p in (2 * g, 2 * g + 1):
                xs_p = xc[sl, p * LANES:(p + 1) * LANES]
                cols, tots, dcols = [], [], []
                y_p = None
                for hh in range(2):
                    dh = direction * SSD_HEADS + 2 * p + hh
                    col = cum[:, dh:dh + 1]
                    cols.append(col)
                    tots.append(tot[:, dh:dh + 1])
                    dcols.append(dt_s[:, dh:dh + 1])
                    diff = jnp.where(tri, col - cum_t[dh:dh + 1, :], NEG)
                    gmat = sg * jnp.exp(diff) * dt_t[dh:dh + 1, :]
                    half = lo if hh == 0 else jnp.logical_not(lo)
                    part = _dot(gmat.astype(BF16), jnp.where(half, xs_p, 0.0).astype(BF16))
                    y_p = part if y_p is None else y_p + part
                h_old = h_scr[p]
                e_col = jnp.where(lo, jnp.exp(cols[0]), jnp.exp(cols[1]))
                y_p = y_p + _dot((cg * e_col).astype(BF16), h_old.astype(BF16))
                w_col = jnp.where(lo, jnp.exp(tots[0] - cols[0]) * dcols[0],
                                  jnp.exp(tots[1] - cols[1]) * dcols[1])
                dh_new = _dot_tn((bg * w_col).astype(BF16), xs_p.astype(BF16))
                dec = jnp.where(rlo, jnp.exp(tots[0]), jnp.exp(tots[1]))
                h_scr[p] = h_old * dec + jnp.where(blk, dh_new, 0.0)
                if direction == 0:
                    y_ref[sl, p * LANES:(p + 1) * LANES] = y_p
                else:
                    yb_scr[sl, p * LANES:(p + 1) * LANES] = y_p

    if direction == 1:
        y = yf_ref[...] + yb_scr[...] + xc[:, :SSD_INNER] * dsk_ref[...]
        y = y * _silu(z_ref[...].astype(F32))
        y_ref[...] = _rms(y, nw_ref[...]).astype(y_ref.dtype)


def _ssd_call(direction, xbc, dt, cw, cb, dtb, aneg, extra=None):
    bsz, s_tot, _ = xbc.shape
    nt = s_tot // TILE
    prev_idx, next_idx = _halo_maps(direction, nt)
    tmap = lambda b, c: (b, _scan_tile(direction, c, nt), 0)
    in_specs = [
        pl.BlockSpec((None, TILE, SSD_CONV_DIM), tmap),
        pl.BlockSpec((None, SUBLANES, SSD_CONV_DIM), prev_idx),
        pl.BlockSpec((None, SUBLANES, SSD_CONV_DIM), next_idx),
        pl.BlockSpec((None, TILE, LANES), tmap),
        _const_spec((3, SSD_CONV_DIM)),
        _const_spec((1, SSD_CONV_DIM)),
        _const_spec((1, LANES)),
        _const_spec((1, LANES)),
    ]
    args = [xbc, xbc, xbc, dt, cw, cb, dtb, aneg]
    scratch = [pltpu.VMEM((SSD_HEADS // 2, LANES, LANES), F32)]
    if direction == 0:
        out_dtype = F32
    else:
        yf, z, dsk, nw = extra
        in_specs += [
            pl.BlockSpec((None, TILE, SSD_INNER), tmap),
            pl.BlockSpec((None, TILE, SSD_INNER), tmap),
            _const_spec((1, SSD_INNER)),
            _const_spec((1, SSD_INNER)),
        ]
        args += [yf, z, dsk, nw]
        scratch.append(pltpu.VMEM((TILE, SSD_INNER), F32))
        out_dtype = BF16
    return pl.pallas_call(
        functools.partial(_ssd_kernel, direction),
        grid=(bsz, nt),
        in_specs=in_specs,
        out_specs=pl.BlockSpec((None, TILE, SSD_INNER), tmap),
        out_shape=jax.ShapeDtypeStruct((bsz, s_tot, SSD_INNER), out_dtype),
        scratch_shapes=scratch,
        compiler_params=_params(2, carried=True),
        name="ssd_fwd" if direction == 0 else "ssd_bwd",
    )(*args)


def _hg_kernel(direction, *refs):
    if direction == 0:
        (q_ref, f_ref, i_ref, llb_ref, l1m_ref, o_ref,
         st_scr, q_s, v_s, cum_s, ck_s) = refs
        acc_ref = o_ref
    else:
        (q_ref, f_ref, i_ref, llb_ref, l1m_ref, of_ref, g_ref, nw_ref, o_ref,
         st_scr, q_s, v_s, cum_s, ck_s, ob_s) = refs
        acc_ref = ob_s
    c = pl.program_id(1)

    @pl.when(c == 0)
    def _():
        st_scr[...] = jnp.zeros_like(st_scr)

    x = f_ref[...]
    ls = _log_sigmoid(x)
    a = llb_ref[...]
    b = l1m_ref[...] + ls
    m = jnp.maximum(a, b)
    logf = m + jnp.log(1.0 + jnp.exp(jnp.minimum(a, b) - m))
    logk = l1m_ref[...] + (ls - x)

    r_i = _iota((TILE, TILE), 0)
    c_i = _iota((TILE, TILE), 1)
    same = (r_i // HG_BLOCK) == (c_i // HG_BLOCK)
    if direction == 0:
        tri = jnp.logical_and(same, c_i <= r_i)
    else:
        tri = jnp.logical_and(same, c_i >= r_i)
    cum = _dot01(tri.astype(BF16), logf)
    cum_s[...] = cum
    ck_s[...] = cum - logk
    q_s[...] = _silu(q_ref[...].astype(F32))
    v_s[...] = i_ref[...].astype(F32)

    t_i = _iota((HG_BLOCK, 1), 0)
    ones2 = ((_iota((LANES, LANES), 0) // HG_HEAD) == (_iota((LANES, LANES), 1) // HG_HEAD))
    ones2_b = ones2.astype(BF16)
    nblk = TILE // HG_BLOCK
    tot_row = HG_BLOCK - 1 if direction == 0 else 0

    def body(it, carry):
        blk_i = it if direction == 0 else nblk - 1 - it
        r0 = pl.multiple_of(blk_i * HG_BLOCK, HG_BLOCK)
        for p in range(HG_INNER // LANES):
            ls_ = pl.ds(p * LANES, LANES)
            qb = q_s[pl.ds(r0, HG_BLOCK), ls_]
            vb = v_s[pl.ds(r0, HG_BLOCK), ls_]
            cb = cum_s[pl.ds(r0, HG_BLOCK), ls_]
            kb = ck_s[pl.ds(r0, HG_BLOCK), ls_]
            tot = cb[tot_row:tot_row + 1, :]
            ws = []
            for s in range(HG_BLOCK):
                live = (t_i >= s) if direction == 0 else (t_i <= s)
                diff = jnp.where(live, cb - kb[s:s + 1, :], NEG)
                ws.append((qb * jnp.exp(diff)).astype(BF16))
            red = _dot(jnp.concatenate(ws, axis=0), ones2_b)
            o_blk = None
            for s in range(HG_BLOCK):
                term = red[s * HG_BLOCK:(s + 1) * HG_BLOCK] * vb[s:s + 1, :]
                o_blk = term if o_blk is None else o_blk + term
            st_old = st_scr[p]
            o_blk = o_blk + _dot_nt((qb * jnp.exp(cb)).astype(BF16), st_old.astype(BF16))
            khat = jnp.exp(tot - kb)
            delta = _dot_tn(vb.astype(BF16), khat.astype(BF16))
            st_scr[p] = st_old * jnp.exp(tot) + jnp.where(ones2, delta, 0.0)
            acc_ref[pl.ds(r0, HG_BLOCK), ls_] = o_blk
        return carry

    lax.fori_loop(0, nblk, body, 0)

    if direction == 1:
        o = of_ref[...] + ob_s[...]
        seg = ((_iota((HG_INNER, HG_INNER), 0) // HG_HEAD)
               == (_iota((HG_INNER, HG_INNER), 1) // HG_HEAD)).astype(BF16)
        ms = _dot01_r(o * o, seg) * (1.0 / HG_HEAD)
        o = o * lax.rsqrt(ms + EPS) * nw_ref[...]
        o_ref[...] = (o * _silu(g_ref[...].astype(F32))).astype(o_ref.dtype)


def _hg_call(direction, hq, hf, hi, log_lb, log_1mlb, extra=None):
    bsz, s_tot, _ = hq.shape
    nt = s_tot // TILE
    tmap = lambda b, c: (b, _scan_tile(direction, c, nt), 0)
    fmap = lambda b, c: (b, _scan_tile(direction, c, nt), direction)
    tok = pl.BlockSpec((None, TILE, HG_INNER), tmap)
    in_specs = [tok, pl.BlockSpec((None, TILE, HG_INNER), fmap), tok,
                _const_spec((1, HG_INNER)), _const_spec((1, HG_INNER))]
    args = [hq, hf, hi, log_lb, log_1mlb]
    scratch = [pltpu.VMEM((HG_INNER // LANES, LANES, LANES), F32)]
    scratch += [pltpu.VMEM((TILE, HG_INNER), F32) for _ in range(4)]
    if direction == 0:
        out_dtype = F32
    else:
        of, hg, nw = extra
        in_specs += [tok, tok, _const_spec((1, HG_INNER))]
        args += [of, hg, nw]
        scratch.append(pltpu.VMEM((TILE, HG_INNER), F32))
        out_dtype = BF16
    return pl.pallas_call(
        functools.partial(_hg_kernel, direction),
        grid=(bsz, nt),
        in_specs=in_specs,
        out_specs=tok,
        out_shape=jax.ShapeDtypeStruct((bsz, s_tot, HG_INNER), out_dtype),
        scratch_shapes=scratch,
        compiler_params=_params(2, carried=True),
        name="hgrn_fwd" if direction == 0 else "hgrn_bwd",
    )(*args)


def _na_kernel(first_tile, grid_rows, q_ref, k_ref, v_ref, bias_ref, o_ref):
    t = pl.program_id(1) + first_tile
    qrows = TILE // GRID_W
    nq = NA_HEADS * GRID_W
    hmask = (_iota((nq, NA_INNER), 0) // GRID_W) == (_iota((nq, NA_INNER), 1) // NA_HEAD_DIM)
    omask = [(_iota((1, NA_INNER), 1) // NA_HEAD_DIM) == h for h in range(NA_HEADS)]
    kc = k_ref[0:TILE, :]
    vc = v_ref[0:TILE, :]
    nloc = NA_ROWS * GRID_W

    def attend(ii, local):
        qrow = q_ref[ii * GRID_W:(ii + 1) * GRID_W, :]
        qb = jnp.where(hmask, jnp.concatenate([qrow] * NA_HEADS, axis=0), 0).astype(BF16)
        s_c = _dot_nt(qb, kc)
        m = jnp.max(s_c, axis=-1, keepdims=True)
        if local:
            i = (t - 1) * qrows + ii
            r0 = jnp.clip(i - NA_ROWS // 2, 0, grid_rows - NA_ROWS)
            cls = r0 - i + (NA_ROWS - 1)
            start = pl.multiple_of(TILE + r0 * GRID_W, GRID_W)
            kl = k_ref[pl.ds(start, nloc), :]
            vl = v_ref[pl.ds(start, nloc), :]
            s_l = _dot_nt(qb, kl) + bias_ref[cls]
            m = jnp.maximum(m, jnp.max(s_l, axis=-1, keepdims=True))
            p_l = jnp.exp(s_l - m)
        p_c = jnp.exp(s_c - m)
        den = jnp.sum(p_c, axis=-1, keepdims=True)
        o = _dot(p_c.astype(BF16), vc)
        if local:
            den = den + jnp.sum(p_l, axis=-1, keepdims=True)
            o = o + _dot(p_l.astype(BF16), vl)
        o = o / den
        out = None
        for h in range(NA_HEADS):
            part = jnp.where(omask[h], o[h * GRID_W:(h + 1) * GRID_W, :], 0.0)
            out = part if out is None else out + part
        o_ref[ii * GRID_W:(ii + 1) * GRID_W, :] = out.astype(o_ref.dtype)

    @pl.when(t == 0)
    def _():
        for ii in range(qrows):
            attend(ii, False)

    @pl.when(t != 0)
    def _():
        for ii in range(qrows):
            attend(ii, True)


def _na_call(q, k, v, bias, need_ctx):
    bsz, s_tot, _ = q.shape
    nt = s_tot // TILE
    first = 0 if need_ctx else 1
    grid_rows = (s_tot - TILE) // GRID_W
    ncls = bias.shape[0]
    full = pl.BlockSpec((None, s_tot, NA_INNER), lambda b, t: (b, 0, 0))
    tok = pl.BlockSpec((None, TILE, NA_INNER), lambda b, t: (b, t + first, 0))
    return pl.pallas_call(
        functools.partial(_na_kernel, first, grid_rows),
        grid=(bsz, nt - first),
        in_specs=[tok, full, full,
                  _const_spec((ncls, NA_HEADS * GRID_W, NA_ROWS * GRID_W))],
        out_specs=tok,
        out_shape=jax.ShapeDtypeStruct((bsz, s_tot, NA_INNER), BF16),
        compiler_params=_params(2),
        name="natten",
    )(q, k, v, bias)


def _na_bias_table(rpb):
    j = np.arange(GRID_W)[:, None]
    kcol = np.arange(GRID_W)[None, :]
    c0 = np.clip(j - NA_COLS // 2, 0, GRID_W - NA_COLS)
    inwin = (kcol >= c0) & (kcol < c0 + NA_COLS)
    cidx = np.clip(kcol - j + NA_COLS - 1, 0, 2 * NA_COLS - 2)
    cls = np.arange(NA_ROWS)[:, None]
    ridx = cls + np.arange(NA_ROWS)[None, :]
    tab = rpb.astype(F32)[:, ridx][:, :, :, cidx]
    tab = jnp.where(jnp.asarray(inwin)[None, None, None], tab, NEG)
    tab = tab.transpose(1, 0, 3, 2, 4)
    return tab.reshape(NA_ROWS, NA_HEADS * GRID_W, NA_ROWS * GRID_W)


def _mixout_kernel(h_ref, ssd_ref, na_ref, hg_ref, wo_ref, gate_ref, nw_ref, o_ref):
    a = _dot(ssd_ref[...], wo_ref[0:SSD_INNER, :])
    a = a + _dot(na_ref[...], wo_ref[SSD_INNER:SSD_INNER + NA_INNER, :])
    a = a + _dot(hg_ref[...], wo_ref[SSD_INNER + NA_INNER:, :])
    o_ref[...] = h_ref[...] + gate_ref[...] * _rms(a, nw_ref[...])


def _mixout_call(h, ssd, na, hg, wo, gate, nw, need_ctx, n_b):
    bsz, s_tot, d = h.shape
    nt = s_tot // TILE
    first = 0 if need_ctx else 1
    mod = _mod_spec(first)
    tok = lambda width: pl.BlockSpec((None, TILE, width), lambda b, t: (b, t + first, 0))
    return pl.pallas_call(
        _mixout_kernel,
        grid=(bsz, nt - first),
        in_specs=[tok(d), tok(SSD_INNER), tok(NA_INNER), tok(HG_INNER),
                  _const_spec(wo.shape),
                  pl.BlockSpec((None, 1, d), lambda b, t: mod(b, t, n_b)),
                  _const_spec((1, d))],
        out_specs=tok(d),
        out_shape=jax.ShapeDtypeStruct((bsz, s_tot, d), F32),
        compiler_params=_params(2),
        name="mixout",
    )(h, ssd, na, hg, wo, gate, nw)


def _ffn_kernel(first_tile, h_ref, hp_ref, hn_ref, sh_ref, sc_ref, gate_ref, nwa_ref, nwb_ref,
                wg_ref, wu_ref, cw_ref, cb_ref, wd_ref, o_ref):
    t = pl.program_id(1) + first_tile
    nt = pl.num_programs(1) + first_tile
    has_prev, has_next = _seq_edges(t, nt)
    h = h_ref[...]
    hext = jnp.concatenate([hp_ref[...], h, hn_ref[...]], axis=0)
    v = _rms(hext, nwa_ref[...]) * (1.0 + sc_ref[...]) + sh_ref[...]
    vb = v.astype(BF16)
    gate = _dot(vb, wg_ref[...])
    n_ext = TILE + 2 * SUBLANES
    ridx = _iota((n_ext, 1), 0)
    dead = jnp.logical_or(jnp.logical_and(ridx < SUBLANES, jnp.logical_not(has_prev)),
                          jnp.logical_and(ridx >= SUBLANES + TILE, jnp.logical_not(has_next)))
    gate = jnp.where(dead, 0.0, gate)
    ctr = slice(SUBLANES, SUBLANES + TILE)
    g_prev = pltpu.roll(gate, 1, 0)[ctr]
    g_next = pltpu.roll(gate, n_ext - 1, 0)[ctr]
    gc = cb_ref[...] + cw_ref[0:1, :] * g_prev + cw_ref[1:2, :] * gate[ctr] + cw_ref[2:3, :] * g_next
    up = _dot(vb[ctr], wu_ref[...])
    act = (_silu(gc) * up).astype(BF16)
    f = _dot(act, wd_ref[...])
    o_ref[...] = h + gate_ref[...] * _rms(f, nwb_ref[...])


def _ffn_call(h, shift, scale, gate, nwa, nwb, wg, wu, cw, cb, wd, need_ctx, n_b, lat_only_out):
    bsz, s_tot, d = h.shape
    nt = s_tot // TILE
    first = 0 if need_ctx else 1
    mod = _mod_spec(first)
    rpt = TILE // SUBLANES
    tok = pl.BlockSpec((None, TILE, d), lambda b, t: (b, t + first, 0))

    def prev_idx(b, t):
        tt = t + first
        has = jnp.logical_and(tt != 0, tt != 1)
        return (b, jnp.where(has, tt * rpt - 1, tt * rpt), 0)

    def next_idx(b, t):
        tt = t + first
        has = jnp.logical_and(tt != 0, tt != nt - 1)
        return (b, jnp.where(has, (tt + 1) * rpt, tt * rpt + rpt - 1), 0)

    mspec = pl.BlockSpec((None, 1, d), lambda b, t: mod(b, t, n_b))
    if lat_only_out:
        out_spec = pl.BlockSpec((None, TILE, d), lambda b, t: (b, t, 0))
        out_shape = jax.ShapeDtypeStruct((bsz, s_tot - TILE, d), F32)
    else:
        out_spec = tok
        out_shape = jax.ShapeDtypeStruct((bsz, s_tot, d), F32)
    return pl.pallas_call(
        functools.partial(_ffn_kernel, first),
        grid=(bsz, nt - first),
        in_specs=[tok,
                  pl.BlockSpec((None, SUBLANES, d), prev_idx),
                  pl.BlockSpec((None, SUBLANES, d), next_idx),
                  mspec, mspec, mspec,
                  _const_spec((1, d)), _const_spec((1, d)),
                  _const_spec(wg.shape), _const_spec(wu.shape),
                  _const_spec(cw.shape), _const_spec(cb.shape), _const_spec(wd.shape)],
        out_specs=out_spec,
        out_shape=out_shape,
        compiler_params=_params(2),
        name="convffn",
    )(h, h, h, shift, scale, gate, nwa, nwb, wg, wu, cw, cb, wd)


def kernel(x, c, ctx, c_ctx, ada_w, ada_b, norm_mix_pre, norm_mix_post, norm_ffn_pre, norm_ffn_post,
           w_in, ssd_conv_w, ssd_conv_b, ssd_dt_bias, ssd_a_log, ssd_d, ssd_norm, na_rpb,
           hg_lb_logits, hg_norm, w_out, ffn_w_up, ffn_conv_w, ffn_conv_b, ffn_w_down):
    bsz, seq, d = x.shape
    depth = ada_w.shape[0]
    assert ctx.shape[1] == TILE and seq % TILE == 0 and (seq // GRID_W) >= NA_ROWS

    mrows = -(-(bsz + 1) // SUBLANES) * SUBLANES
    cvec = jnp.zeros((mrows, d), F32).at[:bsz].set(c).at[bsz].set(c_ctx)
    mods = _ada_call(cvec, ada_w, ada_b)
    mods = mods.reshape(depth, mrows, 6, 1, d).transpose(0, 2, 1, 3, 4)

    cuts = np.cumsum(IN_WIDTHS)[:-1].tolist()
    wz, wxbc, wdt, wqkv, whq, whf, whi, whg = jnp.split(w_in, cuts, axis=-1)
    wq, wk, wv = jnp.split(wqkv, 3, axis=-1)
    wdt = jnp.pad(wdt, ((0, 0), (0, 0), (0, LANES - wdt.shape[-1])))
    w_all = jnp.concatenate([wz, wxbc, wdt, wq * (NA_HEAD_DIM ** -0.5), wk, wv, whq, whf, whi, whg],
                            axis=-1).astype(BF16)
    wo_b = w_out.astype(BF16)
    wg_b = ffn_w_up[:, :, :D_FF].astype(BF16)
    wu_b = ffn_w_up[:, :, D_FF:].astype(BF16)
    wd_b = ffn_w_down.astype(BF16)
    pad16 = lambda a: jnp.pad(a.reshape(depth, 1, -1).astype(F32),
                              ((0, 0), (0, 0), (0, LANES - 2 * SSD_HEADS)))
    dtb = pad16(ssd_dt_bias)
    aneg = pad16(-jnp.exp(ssd_a_log.astype(F32)))
    dsk = jnp.repeat(ssd_d.astype(F32), SSD_HEAD_DIM, axis=-1).reshape(depth, 1, SSD_INNER)
    lb = jnp.cumsum(jax.nn.softmax(hg_lb_logits.astype(F32), axis=0), axis=0)
    lb = lb - lb[0]
    log_lb = jnp.log(lb).reshape(depth, 1, HG_INNER)
    log_1mlb = jnp.log1p(-lb).reshape(depth, 1, HG_INNER)
    hg_nw = jnp.tile(hg_norm.astype(F32), (1, HG_INNER // HG_HEAD)).reshape(depth, 1, HG_INNER)
    row = lambda a, l: a[l].reshape(1, -1).astype(F32)

    h = jnp.concatenate([ctx, x], axis=1)
    for l in range(depth):
        need_ctx = l < depth - 1
        p = _inproj_call(h, mods[l, 0], mods[l, 1], row(norm_mix_pre, l), w_all[l], bsz)
        z, xbc, dt, q, k, v, hq, hf, hi, hg = p
        cw, cb = ssd_conv_w[l].astype(F32), row(ssd_conv_b, l)
        yf = _ssd_call(0, xbc, dt, cw, cb, dtb[l], aneg[l])
        ssd = _ssd_call(1, xbc, dt, cw, cb, dtb[l], aneg[l],
                        extra=(yf, z, dsk[l], row(ssd_norm, l)))
        na = _na_call(q, k, v, _na_bias_table(na_rpb[l]), need_ctx)
        of = _hg_call(0, hq, hf, hi, log_lb[l], log_1mlb[l])
        hgo = _hg_call(1, hq, hf, hi, log_lb[l], log_1mlb[l], extra=(of, hg, hg_nw[l]))
        h = _mixout_call(h, ssd, na, hgo, wo_b[l], mods[l, 2], row(norm_mix_post, l), need_ctx, bsz)
        h = _ffn_call(h, mods[l, 3], mods[l, 4], mods[l, 5], row(norm_ffn_pre, l),
                      row(norm_ffn_post, l), wg_b[l], wu_b[l], ffn_conv_w[l].astype(F32),
                      row(ffn_conv_b, l), wd_b[l], need_ctx, bsz, lat_only_out=not need_ctx)
    return h
```
